```python
import jax, jax.numpy as jnp
from jax import lax
import numpy as np

D_MODEL = 4096
BATCH = 1
SEQ = 16384
DEPTH = 4

CHUNK = 64
N_MEM = 256
W_CROSS = D_MODEL // 4
W_CONV = (D_MODEL - W_CROSS) // 2
W_SG = D_MODEL - W_CROSS - W_CONV
CONV_K = 31
CONV_GROUP = 128
CONV_GROUPS = W_CONV // CONV_GROUP
SG_BLOCK = 128
SG_HEAD_DIM = 128
SG_HEADS = W_SG // SG_HEAD_DIM
CROSS_HEADS = 4
CROSS_HEAD_DIM = W_CROSS // CROSS_HEADS
W_IN = 2 * W_CONV + 2 * W_SG + W_CROSS
D_FF = ((8 * D_MODEL + 3 * 256 - 1) // (3 * 256)) * 256
EPS = 1e-6

kernel_name = "hybrid_conv_gmlp_memxattn_trunk"


def _rmsnorm(x, g):
    xf = x.astype(jnp.float32)
    y = xf * lax.rsqrt(jnp.mean(xf * xf, axis=-1, keepdims=True) + EPS)
    return (y * g.astype(jnp.float32)).astype(x.dtype)


def _layernorm(x, g, b):
    xf = x.astype(jnp.float32)
    mu = jnp.mean(xf, axis=-1, keepdims=True)
    xc = xf - mu
    y = xc * lax.rsqrt(jnp.mean(xc * xc, axis=-1, keepdims=True) + EPS)
    return (y * g.astype(jnp.float32) + b.astype(jnp.float32)).astype(x.dtype)


def _causal_depthwise_conv(x, w, b):
    c = x.shape[-1]
    y = lax.conv_general_dilated(
        x, w[:, None, :].astype(x.dtype), window_strides=(1,),
        padding=((CONV_K - 1, 0),), dimension_numbers=("NWC", "WIO", "NWC"),
        feature_group_count=c)
    return y + b


def _conv_module(a, conv_w, conv_b, ln_g, ln_b):
    val, gate = jnp.split(a, 2, axis=-1)
    z = val * jax.nn.sigmoid(gate)
    z = _causal_depthwise_conv(z, conv_w, conv_b)
    return jax.nn.silu(_layernorm(z, ln_g, ln_b))


def _spatial_gating(g, ln_g, ln_b, w_s, b_s):
    bsz, seq, _ = g.shape
    z = jax.nn.gelu(g, approximate=False)
    u, v = jnp.split(z, 2, axis=-1)
    v = _layernorm(v, ln_g, ln_b)
    vb = v.reshape(bsz, seq // SG_BLOCK, SG_BLOCK, SG_HEADS, SG_HEAD_DIM)
    cidx = jnp.arange(SG_BLOCK) // CHUNK
    mask = cidx[None, :] <= cidx[:, None]
    w = jnp.where(mask[None], w_s, jnp.zeros_like(w_s))
    s = jnp.einsum("hij,bnjhd->bnihd", w, vb) + b_s.T[None, None, :, :, None]
    return u * s.reshape(bsz, seq, W_SG)


def _memory_cross_attention(q, mem, mem_g, w_kv):
    bsz, seq, _ = q.shape
    kv = _rmsnorm(mem, mem_g) @ w_kv
    k, v = jnp.split(kv, 2, axis=-1)
    qh = q.reshape(bsz, seq, CROSS_HEADS, CROSS_HEAD_DIM)
    kh = k.reshape(bsz, -1, CROSS_HEADS, CROSS_HEAD_DIM)
    vh = v.reshape(bsz, -1, CROSS_HEADS, CROSS_HEAD_DIM)
    scale = CROSS_HEAD_DIM ** -0.5
    scores = jnp.einsum("bshd,bmhd->bhsm", qh, kh).astype(jnp.float32) * scale
    probs = jax.nn.softmax(scores, axis=-1).astype(vh.dtype)
    o = jnp.einsum("bhsm,bmhd->bshd", probs, vh)
    return o.reshape(bsz, seq, W_CROSS)


def setup_inputs(seed: int = 0) -> dict:
    key = jax.random.key(seed)
    ks = jax.random.split(key, 24)
    f32 = jnp.float32

    def nrm(k, shape, scale):
        return jax.random.normal(k, shape, f32) * scale

    def gain(k, shape):
        return 1.0 + 0.02 * jax.random.normal(k, shape, f32)

    return {
        "x": jax.random.normal(ks[0], (BATCH, SEQ, D_MODEL), f32),
        "mem": jax.random.normal(ks[1], (BATCH, N_MEM, D_MODEL), f32),
        "norm_mix": gain(ks[2], (DEPTH, D_MODEL)),
        "w_in": nrm(ks[3], (DEPTH, D_MODEL, W_IN), D_MODEL ** -0.5),
        "conv_w": nrm(ks[4], (DEPTH, CONV_K, W_CONV), CONV_K ** -0.5),
        "conv_b": nrm(ks[5], (DEPTH, W_CONV), 0.02),
        "conv_ln_g": gain(ks[6], (DEPTH, W_CONV)),
        "conv_ln_b": nrm(ks[7], (DEPTH, W_CONV), 0.02),
        "sg_ln_g": gain(ks[8], (DEPTH, W_SG)),
        "sg_ln_b": nrm(ks[9], (DEPTH, W_SG), 0.02),
        "sg_w": nrm(ks[10], (DEPTH, SG_HEADS, SG_BLOCK, SG_BLOCK), SG_BLOCK ** -0.5),
        "sg_b": gain(ks[11], (DEPTH, SG_HEADS, SG_BLOCK)),
        "mem_norm": gain(ks[12], (DEPTH, D_MODEL)),
        "w_mem_kv": nrm(ks[13], (DEPTH, D_MODEL, 2 * W_CROSS), D_MODEL ** -0.5),
        "out_norm": gain(ks[14], (DEPTH, D_MODEL)),
        "w_out": nrm(ks[15], (DEPTH, D_MODEL, D_MODEL), D_MODEL ** -0.5),
        "norm_ffn": gain(ks[16], (DEPTH, D_MODEL)),
        "w_gate_up": nrm(ks[17], (DEPTH, D_MODEL, 2 * D_FF), D_MODEL ** -0.5),
        "w_down": nrm(ks[18], (DEPTH, D_FF, D_MODEL), D_FF ** -0.5),
        "final_norm": gain(ks[19], (D_MODEL,)),
    }


def reference(x, mem, norm_mix, w_in, conv_w, conv_b, conv_ln_g, conv_ln_b,
              sg_ln_g, sg_ln_b, sg_w, sg_b, mem_norm, w_mem_kv, out_norm, w_out,
              norm_ffn, w_gate_up, w_down, final_norm):
    for l in range(DEPTH):
        h = _rmsnorm(x, norm_mix[l])
        p = h @ w_in[l]
        a = p[..., : 2 * W_CONV]
        g = p[..., 2 * W_CONV: 2 * W_CONV + 2 * W_SG]
        q = p[..., 2 * W_CONV + 2 * W_SG:]
        y_a = _conv_module(a, conv_w[l], conv_b[l], conv_ln_g[l], conv_ln_b[l])
        y_b = _spatial_gating(g, sg_ln_g[l], sg_ln_b[l], sg_w[l], sg_b[l])
        y_c = _memory_cross_attention(q, mem, mem_norm[l], w_mem_kv[l])
        on = out_norm[l]
        y = jnp.concatenate([
            _rmsnorm(y_a, on[:W_CONV]),
            _rmsnorm(y_b, on[W_CONV:W_CONV + W_SG]),
            _rmsnorm(y_c, on[W_CONV + W_SG:]),
        ], axis=-1)
        x = x + y @ w_out[l]
        h2 = _rmsnorm(x, norm_ffn[l])
        gu = h2 @ w_gate_up[l]
        gate, up = jnp.split(gu, 2, axis=-1)
        x = x + (jax.nn.silu(gate) * up) @ w_down[l]
    return _rmsnorm(x, final_norm)
```

```python
import functools
import math

import jax
import jax.numpy as jnp
from jax import lax
from jax.experimental import pallas as pl
from jax.experimental.pallas import tpu as pltpu

EPS = 1e-6
CHUNK = 64
SG_BLOCK = 128
SG_HEAD_DIM = 128
CONV_K = 31
CONV_HALO = 32
CONV_ROWS = 64
LANES = 128
CROSS_HEADS = 4
VMEM_LIMIT_BYTES = 56 * 1024 * 1024

F32 = jnp.float32
BF16 = jnp.bfloat16


def _params(*semantics):
    return pltpu.CompilerParams(dimension_semantics=semantics,
                                vmem_limit_bytes=VMEM_LIMIT_BYTES)


def _pick(n, candidates):
    for c in candidates:
        if n % c == 0:
            return c
    return n


def _rms_scale(x, n):
    return lax.rsqrt(jnp.sum(x * x, axis=-1, keepdims=True) * (1.0 / n) + EPS)


NORM_ROWS = 32


def _norm_rows_into(x_ref, g_ref, h_ref):
    tm, k = x_ref.shape

    def body(r, carry):
        r0 = pl.multiple_of(r * NORM_ROWS, NORM_ROWS)
        xs = x_ref[pl.ds(r0, NORM_ROWS), :]
        h_ref[pl.ds(r0, NORM_ROWS), :] = (xs * _rms_scale(xs, k) * g_ref[...]).astype(BF16)
        return carry

    lax.fori_loop(0, tm // NORM_ROWS, body, 0)


def _norm_matmul_kernel(x_ref, g_ref, w_ref, o_ref, h_ref):
    @pl.when(pl.program_id(1) == 0)
    def _():
        _norm_rows_into(x_ref, g_ref, h_ref)

    o_ref[...] = jnp.dot(h_ref[...], w_ref[...], preferred_element_type=F32).astype(o_ref.dtype)


def _norm_matmul(x, g, w, out_dtype, tm, tn):
    s, k = x.shape
    n = w.shape[1]
    return pl.pallas_call(
        _norm_matmul_kernel,
        grid=(s // tm, n // tn),
        in_specs=[
            pl.BlockSpec((tm, k), lambda i, j: (i, 0)),
            pl.BlockSpec((1, k), lambda i, j: (0, 0)),
            pl.BlockSpec((k, tn), lambda i, j: (0, j)),
        ],
        out_specs=pl.BlockSpec((tm, tn), lambda i, j: (i, j)),
        out_shape=jax.ShapeDtypeStruct((s, n), out_dtype),
        scratch_shapes=[pltpu.VMEM((tm, k), BF16)],
        compiler_params=_params("parallel", "arbitrary"),
        name="norm_matmul",
    )(x, g.reshape(1, k), w)


def _norm_swiglu_kernel(x_ref, g_ref, wg_ref, wu_ref, o_ref, h_ref):
    @pl.when(pl.program_id(1) == 0)
    def _():
        _norm_rows_into(x_ref, g_ref, h_ref)

    h = h_ref[...]
    gate = jnp.dot(h, wg_ref[...], preferred_element_type=F32)
    up = jnp.dot(h, wu_ref[...], preferred_element_type=F32)
    o_ref[...] = (gate * jax.nn.sigmoid(gate) * up).astype(o_ref.dtype)


def _norm_swiglu(x, g, w_gate_up, d_ff, tm, tn):
    s, k = x.shape
    nj = d_ff // tn
    return pl.pallas_call(
        _norm_swiglu_kernel,
        grid=(s // tm, nj),
        in_specs=[
            pl.BlockSpec((tm, k), lambda i, j: (i, 0)),
            pl.BlockSpec((1, k), lambda i, j: (0, 0)),
            pl.BlockSpec((k, tn), lambda i, j: (0, j)),
            pl.BlockSpec((k, tn), lambda i, j: (0, j + nj)),
        ],
        out_specs=pl.BlockSpec((tm, tn), lambda i, j: (i, j)),
        out_shape=jax.ShapeDtypeStruct((s, d_ff), BF16),
        scratch_shapes=[pltpu.VMEM((tm, k), BF16)],
        compiler_params=_params("parallel", "arbitrary"),
        name="norm_swiglu",
    )(x, g.reshape(1, k), w_gate_up, w_gate_up)


def _matmul_residual_kernel(a_ref, w_ref, x_ref, o_ref):
    o_ref[...] = x_ref[...] + jnp.dot(a_ref[...], w_ref[...], preferred_element_type=F32)


def _matmul_residual(a, w, x, tm, tn):
    s, k = a.shape
    n = w.shape[1]
    return pl.pallas_call(
        _matmul_residual_kernel,
        grid=(s // tm, n // tn),
        in_specs=[
            pl.BlockSpec((tm, k), lambda i, j: (i, 0)),
            pl.BlockSpec((k, tn), lambda i, j: (0, j)),
            pl.BlockSpec((tm, tn), lambda i, j: (i, j)),
        ],
        out_specs=pl.BlockSpec((tm, tn), lambda i, j: (i, j)),
        out_shape=jax.ShapeDtypeStruct((s, n), F32),
        input_output_aliases={2: 0},
        compiler_params=_params("parallel", "parallel"),
        name="matmul_residual",
    )(a, w, x)


def _rmsnorm_kernel(x_ref, g_ref, o_ref):
    x = x_ref[...]
    o_ref[...] = x * _rms_scale(x, x.shape[-1]) * g_ref[...]


def _rmsnorm(x, g, tm):
    s, k = x.shape
    return pl.pallas_call(
        _rmsnorm_kernel,
        grid=(s // tm,),
        in_specs=[pl.BlockSpec((tm, k), lambda i: (i, 0)),
                  pl.BlockSpec((1, k), lambda i: (0, 0))],
        out_specs=pl.BlockSpec((tm, k), lambda i: (i, 0)),
        out_shape=jax.ShapeDtypeStruct((s, k), F32),
        compiler_params=_params("parallel"),
        name="final_rmsnorm",
    )(x, g.reshape(1, k))


def _layernorm(x, g, b):
    n = x.shape[-1]
    mu = jnp.sum(x, axis=-1, keepdims=True) * (1.0 / n)
    xc = x - mu
    var = jnp.sum(xc * xc, axis=-1, keepdims=True) * (1.0 / n)
    return xc * lax.rsqrt(var + EPS) * g + b


def _mixer_kernel(p_ref, halo_ref, cw_ref, cb_ref, clg_ref, clb_ref,
                  slg_ref, slb_ref, sw_ref, sbt_ref, kt_ref, v_ref, on_ref,
                  y_ref, zh_ref, cv_ref, *, w_conv, w_sg, w_cross):
    rows = p_ref.shape[0]
    n_cc = w_conv // LANES

    not_first = (pl.program_id(0) > 0).astype(F32)
    for c in range(n_cc):
        lo = c * LANES
        val = p_ref[:, lo:lo + LANES]
        gate = p_ref[:, w_conv + lo:w_conv + lo + LANES]
        zh_ref[c, CONV_HALO:CONV_HALO + rows, :] = val * jax.nn.sigmoid(gate)
        hval = halo_ref[:, lo:lo + LANES]
        hgate = halo_ref[:, w_conv + lo:w_conv + lo + LANES]
        zh_ref[c, 0:CONV_HALO, :] = hval * jax.nn.sigmoid(hgate) * not_first
    first_tap = CONV_HALO - (CONV_K - 1)
    for c in range(n_cc):
        lo = c * LANES
        for r0 in range(0, rows, CONV_ROWS):
            win = zh_ref[c, r0:r0 + CONV_ROWS + CONV_HALO, :]
            acc = jnp.broadcast_to(cb_ref[:, lo:lo + LANES], (CONV_ROWS, LANES))
            for k in range(CONV_K):
                acc = acc + cw_ref[k:k + 1, lo:lo + LANES] * win[first_tap + k:first_tap + k + CONV_ROWS, :]
            cv_ref[r0:r0 + CONV_ROWS, lo:lo + LANES] = acc
    ya = _layernorm(cv_ref[...], clg_ref[...], clb_ref[...])
    ya = ya * jax.nn.sigmoid(ya)
    y_ref[:, 0:w_conv] = (ya * _rms_scale(ya, w_conv) * on_ref[:, 0:w_conv]).astype(y_ref.dtype)

    g0 = 2 * w_conv
    gz = p_ref[:, g0:g0 + 2 * w_sg]
    gz = 0.5 * gz * (1.0 + lax.erf(gz * (1.0 / math.sqrt(2.0))))
    u = gz[:, :w_sg]
    v = _layernorm(gz[:, w_sg:], slg_ref[...], slb_ref[...]).astype(BF16)
    chunk_shift = CHUNK.bit_length() - 1
    ci = lax.shift_right_logical(lax.broadcasted_iota(jnp.int32, (rows, rows), 0), chunk_shift)
    cj = lax.shift_right_logical(lax.broadcasted_iota(jnp.int32, (rows, rows), 1), chunk_shift)
    causal = cj <= ci
    for h in range(w_sg // SG_HEAD_DIM):
        lo = h * SG_HEAD_DIM
        w = jnp.where(causal, sw_ref[h], 0.0).astype(BF16)
        s = jnp.dot(w, v[:, lo:lo + SG_HEAD_DIM], preferred_element_type=F32) + sbt_ref[:, h:h + 1]
        cv_ref[:, lo:lo + SG_HEAD_DIM] = u[:, lo:lo + SG_HEAD_DIM] * s
    yb = cv_ref[...]
    y_ref[:, w_conv:w_conv + w_sg] = (
        yb * _rms_scale(yb, w_sg) * on_ref[:, w_conv:w_conv + w_sg]).astype(y_ref.dtype)

    q0 = g0 + 2 * w_sg
    hd = w_cross // CROSS_HEADS
    scale = hd ** -0.5
    outs = []
    for h in range(CROSS_HEADS):
        q = p_ref[:, q0 + h * hd:q0 + (h + 1) * hd].astype(BF16)
        sc = jnp.dot(q, kt_ref[h], preferred_element_type=F32) * scale
        e = jnp.exp(sc - jnp.max(sc, axis=-1, keepdims=True))
        probs = (e / jnp.sum(e, axis=-1, keepdims=True)).astype(BF16)
        outs.append(jnp.dot(probs, v_ref[h], preferred_element_type=F32))
    yc = jnp.concatenate(outs, axis=-1)
    y_ref[:, w_conv + w_sg:] = (
        yc * _rms_scale(yc, w_cross) * on_ref[:, w_conv + w_sg:]).astype(y_ref.dtype)


def _mixer(p, conv_w, conv_b, conv_ln_g, conv_ln_b, sg_ln_g, sg_ln_b, sg_w, sg_bt,
           kt, vh, out_norm):
    s, w_in = p.shape
    w_conv = conv_w.shape[1]
    w_sg = sg_ln_g.shape[0]
    w_cross = w_in - 2 * w_conv - 2 * w_sg
    d = w_conv + w_sg + w_cross
    rows = SG_BLOCK
    halo_blocks = rows // CONV_HALO
    n_heads = sg_w.shape[0]
    n_mem = vh.shape[1]
    hd = w_cross // CROSS_HEADS
    full = lambda *shape: pl.BlockSpec(shape, lambda i: (0,) * len(shape))
    kern = functools.partial(_mixer_kernel, w_conv=w_conv, w_sg=w_sg, w_cross=w_cross)
    return pl.pallas_call(
        kern,
        grid=(s // rows,),
        in_specs=[
            pl.BlockSpec((rows, w_in), lambda i: (i, 0)),
            pl.BlockSpec((CONV_HALO, 2 * w_conv), lambda i: (jnp.maximum(i * halo_blocks - 1, 0), 0)),
            full(CONV_K, w_conv), full(1, w_conv), full(1, w_conv), full(1, w_conv),
            full(1, w_sg), full(1, w_sg),
            full(n_heads, rows, rows), full(rows, n_heads),
            full(CROSS_HEADS, hd, n_mem), full(CROSS_HEADS, n_mem, hd),
            full(1, d),
        ],
        out_specs=pl.BlockSpec((rows, d), lambda i: (i, 0)),
        out_shape=jax.ShapeDtypeStruct((s, d), BF16),
        scratch_shapes=[
            pltpu.VMEM((w_conv // LANES, rows + CONV_HALO, LANES), F32),
            pltpu.VMEM((rows, max(w_conv, w_sg)), F32),
        ],
        compiler_params=_params("parallel"),
        name="mixer",
    )(p, p, conv_w, conv_b.reshape(1, -1), conv_ln_g.reshape(1, -1), conv_ln_b.reshape(1, -1),
      sg_ln_g.reshape(1, -1), sg_ln_b.reshape(1, -1), sg_w, sg_bt, kt, vh, out_norm.reshape(1, -1))


def kernel(x, mem, norm_mix, w_in, conv_w, conv_b, conv_ln_g, conv_ln_b, sg_ln_g, sg_ln_b,
           sg_w, sg_b, mem_norm, w_mem_kv, out_norm, w_out, norm_ffn, w_gate_up, w_down,
           final_norm):
    bsz, seq, d = x.shape
    depth = norm_mix.shape[0]
    n_mem = mem.shape[1]
    d_ff = w_down.shape[1]
    w_cross = w_mem_kv.shape[2] // 2
    hd = w_cross // CROSS_HEADS
    assert bsz == 1 and seq % SG_BLOCK == 0 and sg_w.shape[2] == SG_BLOCK

    tm = _pick(seq, (512, 256, 128))
    xs = x.reshape(seq, d)
    mems = mem.reshape(n_mem, d)
    for l in range(depth):
        w_in_b = w_in[l].astype(BF16)
        w_kv_b = w_mem_kv[l].astype(BF16)
        w_out_b = w_out[l].astype(BF16)
        w_gu_b = w_gate_up[l].astype(BF16)
        w_dn_b = w_down[l].astype(BF16)

        kv = _norm_matmul(mems, mem_norm[l], w_kv_b, BF16, n_mem, _pick(2 * w_cross, (1024, 512, 256)))
        kt = kv[:, :w_cross].reshape(n_mem, CROSS_HEADS, hd).transpose(1, 2, 0)
        vh = kv[:, w_cross:].reshape(n_mem, CROSS_HEADS, hd).transpose(1, 0, 2)

        p = _norm_matmul(xs, norm_mix[l], w_in_b, F32, tm, _pick(w_in_b.shape[1], (1024, 512, 256)))
        y = _mixer(p, conv_w[l], conv_b[l], conv_ln_g[l], conv_ln_b[l], sg_ln_g[l], sg_ln_b[l],
                   sg_w[l], sg_b[l].T, kt, vh, out_norm[l])
        xs = _matmul_residual(y, w_out_b, xs, _pick(seq, (1024, 512, 256, 128)), _pick(d, (512, 256)))
        act = _norm_swiglu(xs, norm_ffn[l], w_gu_b, d_ff, tm, _pick(d_ff, (256, 128)))
        xs = _matmul_residual(act, w_dn_b, xs, tm, _pick(d, (256, 128)))
    out = _rmsnorm(xs, final_norm, _pick(seq, (256, 128)))
    return out.reshape(bsz, seq, d)
```

```python
import functools
import math

import jax
import jax.numpy as jnp
from jax import lax
from jax.experimental import pallas as pl
from jax.experimental.pallas import tpu as pltpu

EPS = 1e-6
CHUNK = 64
SG_BLOCK = 128
SG_HEAD_DIM = 128
CONV_K = 31
CONV_HALO = 32
CONV_ROWS = 64
LANES = 128
CROSS_HEADS = 4
VMEM_LIMIT_BYTES = 56 * 1024 * 1024

F32 = jnp.float32
BF16 = jnp.bfloat16


def _params(*semantics):
    return pltpu.CompilerParams(dimension_semantics=semantics,
                                vmem_limit_bytes=VMEM_LIMIT_BYTES)


def _pick(n, candidates):
    for c in candidates:
        if n % c == 0:
            return c
    return n


def _rms_scale(x, n):
    return lax.rsqrt(jnp.sum(x * x, axis=-1, keepdims=True) * (1.0 / n) + EPS)


def _lane_partial_sumsq(x):
    acc = x[:, 0:LANES] * x[:, 0:LANES]
    for c in range(LANES, x.shape[1], LANES):
        acc = acc + x[:, c:c + LANES] * x[:, c:c + LANES]
    return acc


def _row_scale(ss_ref, n):
    return lax.rsqrt(jnp.sum(ss_ref[...], axis=-1, keepdims=True) * (1.0 / n) + EPS)


def _norm_inputs_kernel(x_ref, xb_ref, ss_ref):
    x = x_ref[...]
    xb_ref[...] = x.astype(BF16)
    ss_ref[...] = _lane_partial_sumsq(x)


def _norm_inputs(x, tm):
    s, k = x.shape
    return pl.pallas_call(
        _norm_inputs_kernel,
        grid=(s // tm,),
        in_specs=[pl.BlockSpec((tm, k), lambda i: (i, 0))],
        out_specs=[pl.BlockSpec((tm, k), lambda i: (i, 0)),
                   pl.BlockSpec((tm, LANES), lambda i: (i, 0))],
        out_shape=[jax.ShapeDtypeStruct((s, k), BF16),
                   jax.ShapeDtypeStruct((s, LANES), F32)],
        compiler_params=_params("parallel"),
        name="norm_inputs",
    )(x)


def _scaled_matmul_kernel(xb_ref, ss_ref, w_ref, o_ref):
    acc = jnp.dot(xb_ref[...], w_ref[...], preferred_element_type=F32)
    o_ref[...] = (acc * _row_scale(ss_ref, xb_ref.shape[1])).astype(o_ref.dtype)


def _scaled_matmul(xb, ss, w_stack, layer, out_dtype, tm, tn):
    s, k = xb.shape
    n = w_stack.shape[2]
    return pl.pallas_call(
        _scaled_matmul_kernel,
        grid=(s // tm, n // tn),
        in_specs=[
            pl.BlockSpec((tm, k), lambda i, j: (i, 0)),
            pl.BlockSpec((tm, LANES), lambda i, j: (i, 0)),
            pl.BlockSpec((None, k, tn), lambda i, j: (layer, 0, j)),
        ],
        out_specs=pl.BlockSpec((tm, tn), lambda i, j: (i, j)),
        out_shape=jax.ShapeDtypeStruct((s, n), out_dtype),
        compiler_params=_params("parallel", "parallel"),
        name="scaled_matmul",
    )(xb, ss, w_stack)


def _scaled_swiglu_kernel(xb_ref, ss_ref, wg_ref, wu_ref, o_ref):
    xb = xb_ref[...]
    r = _row_scale(ss_ref, xb.shape[1])
    gate = jnp.dot(xb, wg_ref[...], preferred_element_type=F32) * r
    up = jnp.dot(xb, wu_ref[...], preferred_element_type=F32) * r
    o_ref[...] = (gate * jax.nn.sigmoid(gate) * up).astype(o_ref.dtype)


def _scaled_swiglu(xb, ss, w_gate_up_stack, layer, d_ff, tm, tn):
    s, k = xb.shape
    nj = d_ff // tn
    return pl.pallas_call(
        _scaled_swiglu_kernel,
        grid=(s // tm, nj),
        in_specs=[
            pl.BlockSpec((tm, k), lambda i, j: (i, 0)),
            pl.BlockSpec((tm, LANES), lambda i, j: (i, 0)),
            pl.BlockSpec((None, k, tn), lambda i, j: (layer, 0, j)),
            pl.BlockSpec((None, k, tn), lambda i, j: (layer, 0, j + nj)),
        ],
        out_specs=pl.BlockSpec((tm, tn), lambda i, j: (i, j)),
        out_shape=jax.ShapeDtypeStruct((s, d_ff), BF16),
        compiler_params=_params("parallel", "parallel"),
        name="scaled_swiglu",
    )(xb, ss, w_gate_up_stack, w_gate_up_stack)


def _matmul_residual_kernel(a_ref, w_ref, x_ref, o_ref):
    o_ref[...] = x_ref[...] + jnp.dot(a_ref[...], w_ref[...], preferred_element_type=F32)


def _matmul_residual_norm_kernel(a_ref, w_ref, x_ref, o_ref, xb_ref, ss_ref):
    xn = x_ref[...] + jnp.dot(a_ref[...], w_ref[...], preferred_element_type=F32)
    o_ref[...] = xn
    xb_ref[...] = xn.astype(BF16)

    @pl.when(pl.program_id(1) == 0)
    def _():
        ss_ref[...] = jnp.zeros_like(ss_ref)

    ss_ref[...] += _lane_partial_sumsq(xn)


def _matmul_residual(a, w_stack, layer, x, tm, tn, with_norm_inputs, update_in_place=True):
    s, k = a.shape
    n = w_stack.shape[2]
    aliases = {2: 0} if update_in_place else {}
    in_specs = [
        pl.BlockSpec((tm, k), lambda i, j: (i, 0)),
        pl.BlockSpec((None, k, tn), lambda i, j: (layer, 0, j)),
        pl.BlockSpec((tm, tn), lambda i, j: (i, j)),
    ]
    tile = pl.BlockSpec((tm, tn), lambda i, j: (i, j))
    if not with_norm_inputs:
        return pl.pallas_call(
            _matmul_residual_kernel,
            grid=(s // tm, n // tn),
            in_specs=in_specs,
            out_specs=tile,
            out_shape=jax.ShapeDtypeStruct((s, n), F32),
            input_output_aliases=aliases,
            compiler_params=_params("parallel", "parallel"),
            name="matmul_residual",
        )(a, w_stack, x)
    return pl.pallas_call(
        _matmul_residual_norm_kernel,
        grid=(s // tm, n // tn),
        in_specs=in_specs,
        out_specs=[tile, tile, pl.BlockSpec((tm, LANES), lambda i, j: (i, 0))],
        out_shape=[jax.ShapeDtypeStruct((s, n), F32),
                   jax.ShapeDtypeStruct((s, n), BF16),
                   jax.ShapeDtypeStruct((s, LANES), F32)],
        input_output_aliases=aliases,
        compiler_params=_params("parallel", "arbitrary"),
        name="matmul_residual_norm",
    )(a, w_stack, x)


def _rmsnorm_kernel(x_ref, g_ref, o_ref):
    x = x_ref[...]
    o_ref[...] = x * _rms_scale(x, x.shape[-1]) * g_ref[...]


def _rmsnorm(x, g, tm):
    s, k = x.shape
    return pl.pallas_call(
        _rmsnorm_kernel,
        grid=(s // tm,),
        in_specs=[pl.BlockSpec((tm, k), lambda i: (i, 0)),
                  pl.BlockSpec((1, k), lambda i: (0, 0))],
        out_specs=pl.BlockSpec((tm, k), lambda i: (i, 0)),
        out_shape=jax.ShapeDtypeStruct((s, k), F32),
        compiler_params=_params("parallel"),
        name="final_rmsnorm",
    )(x, g.reshape(1, k))


def _layernorm(x, g, b):
    n = x.shape[-1]
    mu = jnp.sum(x, axis=-1, keepdims=True) * (1.0 / n)
    xc = x - mu
    var = jnp.sum(xc * xc, axis=-1, keepdims=True) * (1.0 / n)
    return xc * lax.rsqrt(var + EPS) * g + b


def _mixer_kernel(p_ref, halo_ref, cw_ref, cb_ref, clg_ref, clb_ref,
                  slg_ref, slb_ref, sw_ref, sbt_ref, kt_ref, v_ref, on_ref,
                  y_ref, zh_ref, cv_ref, *, w_conv, w_sg, w_cross):
    rows = p_ref.shape[0]
    n_cc = w_conv // LANES

    not_first = (pl.program_id(0) > 0).astype(F32)
    for c in range(n_cc):
        lo = c * LANES
        val = p_ref[:, lo:lo + LANES]
        gate = p_ref[:, w_conv + lo:w_conv + lo + LANES]
        zh_ref[c, CONV_HALO:CONV_HALO + rows, :] = val * jax.nn.sigmoid(gate)
        hval = halo_ref[:, lo:lo + LANES]
        hgate = halo_ref[:, w_conv + lo:w_conv + lo + LANES]
        zh_ref[c, 0:CONV_HALO, :] = hval * jax.nn.sigmoid(hgate) * not_first
    first_tap = CONV_HALO - (CONV_K - 1)
    for c in range(n_cc):
        lo = c * LANES
        for r0 in range(0, rows, CONV_ROWS):
            acc = jnp.broadcast_to(cb_ref[:, lo:lo + LANES], (CONV_ROWS, LANES))
            for k in range(CONV_K):
                t0 = r0 + first_tap + k
                acc = acc + cw_ref[k:k + 1, lo:lo + LANES] * zh_ref[c, t0:t0 + CONV_ROWS, :]
            cv_ref[r0:r0 + CONV_ROWS, lo:lo + LANES] = acc
    ya = _layernorm(cv_ref[...], clg_ref[...], clb_ref[...])
    ya = ya * jax.nn.sigmoid(ya)
    y_ref[:, 0:w_conv] = (ya * _rms_scale(ya, w_conv) * on_ref[:, 0:w_conv]).astype(y_ref.dtype)

    g0 = 2 * w_conv
    gz = p_ref[:, g0:g0 + 2 * w_sg]
    gz = 0.5 * gz * (1.0 + lax.erf(gz * (1.0 / math.sqrt(2.0))))
    u = gz[:, :w_sg]
    v = _layernorm(gz[:, w_sg:], slg_ref[...], slb_ref[...]).astype(BF16)
    chunk_shift = CHUNK.bit_length() - 1
    ci = lax.shift_right_logical(lax.broadcasted_iota(jnp.int32, (rows, rows), 0), chunk_shift)
    cj = lax.shift_right_logical(lax.broadcasted_iota(jnp.int32, (rows, rows), 1), chunk_shift)
    causal = cj <= ci
    for h in range(w_sg // SG_HEAD_DIM):
        lo = h * SG_HEAD_DIM
        w = jnp.where(causal, sw_ref[h], 0.0).astype(BF16)
        s = jnp.dot(w, v[:, lo:lo + SG_HEAD_DIM], preferred_element_type=F32) + sbt_ref[:, h:h + 1]
        cv_ref[:, lo:lo + SG_HEAD_DIM] = u[:, lo:lo + SG_HEAD_DIM] * s
    yb = cv_ref[...]
    y_ref[:, w_conv:w_conv + w_sg] = (
        yb * _rms_scale(yb, w_sg) * on_ref[:, w_conv:w_conv + w_sg]).astype(y_ref.dtype)

    q0 = g0 + 2 * w_sg
    hd = w_cross // CROSS_HEADS
    scale = hd ** -0.5
    outs = []
    for h in range(CROSS_HEADS):
        q = p_ref[:, q0 + h * hd:q0 + (h + 1) * hd].astype(BF16)
        sc = jnp.dot(q, kt_ref[h], preferred_element_type=F32) * scale
        e = jnp.exp(sc - jnp.max(sc, axis=-1, keepdims=True))
        probs = (e / jnp.sum(e, axis=-1, keepdims=True)).astype(BF16)
        outs.append(jnp.dot(probs, v_ref[h], preferred_element_type=F32))
    yc = jnp.concatenate(outs, axis=-1)
    y_ref[:, w_conv + w_sg:] = (
        yc * _rms_scale(yc, w_cross) * on_ref[:, w_conv + w_sg:]).astype(y_ref.dtype)


def _mixer(p, conv_w, conv_b, conv_ln_g, conv_ln_b, sg_ln_g, sg_ln_b, sg_w, sg_bt,
           kt, vh, out_norm):
    s, w_in = p.shape
    w_conv = conv_w.shape[1]
    w_sg = sg_ln_g.shape[0]
    w_cross = w_in - 2 * w_conv - 2 * w_sg
    d = w_conv + w_sg + w_cross
    rows = SG_BLOCK
    halo_blocks = rows // CONV_HALO
    n_heads = sg_w.shape[0]
    n_mem = vh.shape[1]
    hd = w_cross // CROSS_HEADS
    full = lambda *shape: pl.BlockSpec(shape, lambda i: (0,) * len(shape))
    kern = functools.partial(_mixer_kernel, w_conv=w_conv, w_sg=w_sg, w_cross=w_cross)
    return pl.pallas_call(
        kern,
        grid=(s // rows,),
        in_specs=[
            pl.BlockSpec((rows, w_in), lambda i: (i, 0)),
            pl.BlockSpec((CONV_HALO, 2 * w_conv), lambda i: (jnp.maximum(i * halo_blocks - 1, 0), 0)),
            full(CONV_K, w_conv), full(1, w_conv), full(1, w_conv), full(1, w_conv),
            full(1, w_sg), full(1, w_sg),
            full(n_heads, rows, rows), full(rows, n_heads),
            full(CROSS_HEADS, hd, n_mem), full(CROSS_HEADS, n_mem, hd),
            full(1, d),
        ],
        out_specs=pl.BlockSpec((rows, d), lambda i: (i, 0)),
        out_shape=jax.ShapeDtypeStruct((s, d), BF16),
        scratch_shapes=[
            pltpu.VMEM((w_conv // LANES, rows + CONV_HALO, LANES), F32),
            pltpu.VMEM((rows, max(w_conv, w_sg)), F32),
        ],
        compiler_params=_params("parallel"),
        name="mixer",
    )(p, p, conv_w, conv_b.reshape(1, -1), conv_ln_g.reshape(1, -1), conv_ln_b.reshape(1, -1),
      sg_ln_g.reshape(1, -1), sg_ln_b.reshape(1, -1), sg_w, sg_bt, kt, vh, out_norm.reshape(1, -1))


def kernel(x, mem, norm_mix, w_in, conv_w, conv_b, conv_ln_g, conv_ln_b, sg_ln_g, sg_ln_b,
           sg_w, sg_b, mem_norm, w_mem_kv, out_norm, w_out, norm_ffn, w_gate_up, w_down,
           final_norm):
    bsz, seq, d = x.shape
    depth = norm_mix.shape[0]
    n_mem = mem.shape[1]
    d_ff = w_down.shape[1]
    w_cross = w_mem_kv.shape[2] // 2
    hd = w_cross // CROSS_HEADS
    assert bsz == 1 and seq % SG_BLOCK == 0 and sg_w.shape[2] == SG_BLOCK

    bcast = lambda g: g[:, :, None]
    w_in_b = (w_in * bcast(norm_mix)).astype(BF16)
    w_kv_b = (w_mem_kv * bcast(mem_norm)).astype(BF16)
    w_out_b = w_out.astype(BF16)
    w_gu_b = (w_gate_up * bcast(norm_ffn)).astype(BF16)
    w_dn_b = w_down.astype(BF16)

    tm = _pick(seq, (1024, 512, 256, 128))
    tm_down = _pick(seq, (512, 256, 128))
    xs = x.reshape(seq, d)
    memb, mem_ss = _norm_inputs(mem.reshape(n_mem, d), n_mem)
    xb, ss = _norm_inputs(xs, _pick(seq, (256, 128)))
    for l in range(depth):
        kv = _scaled_matmul(memb, mem_ss, w_kv_b, l, BF16, n_mem, _pick(2 * w_cross, (1024, 512, 256)))
        kt = kv[:, :w_cross].reshape(n_mem, CROSS_HEADS, hd).transpose(1, 2, 0)
        vh = kv[:, w_cross:].reshape(n_mem, CROSS_HEADS, hd).transpose(1, 0, 2)

        p = _scaled_matmul(xb, ss, w_in_b, l, F32, tm, _pick(w_in_b.shape[2], (1024, 512, 256)))
        y = _mixer(p, conv_w[l], conv_b[l], conv_ln_g[l], conv_ln_b[l], sg_ln_g[l], sg_ln_b[l],
                   sg_w[l], sg_b[l].T, kt, vh, out_norm[l])
        xs, xb, ss = _matmul_residual(y, w_out_b, l, xs, tm, _pick(d, (512, 256)), True,
                                      update_in_place=l > 0)
        act = _scaled_swiglu(xb, ss, w_gu_b, l, d_ff, tm, _pick(d_ff, (256, 128)))
        if l + 1 < depth:
            xs, xb, ss = _matmul_residual(act, w_dn_b, l, xs, tm_down, _pick(d, (512, 256)), True)
        else:
            xs = _matmul_residual(act, w_dn_b, l, xs, tm_down, _pick(d, (512, 256)), False)
    out = _rmsnorm(xs, final_norm, _pick(seq, (256, 128)))
    return out.reshape(bsz, seq, d)
```
